```python
import math
import jax, jax.numpy as jnp
from jax import lax
import numpy as np

D_MODEL = 2048
BATCH = 2
SEQ = 4096
DEPTH = 1
DEC_BATCH = 32
DEC_SEQ = 8
PAST_LEN = 16384
PAGE_SIZE = 128

REC_HEADS = 8
REC_DK = 128
REC_DV = 128
REC_WIDTH = REC_HEADS * REC_DK
REC_VWIDTH = REC_HEADS * REC_DV
REC_CHUNK = 64
ATT_HEADS = 8
ATT_DIM = 128
ATT_WIDTH = ATT_HEADS * ATT_DIM
MOBA_BLOCK = 256
MOBA_TOPK = 3
MOBA_QUERY_ROWS = 32
PROJ_COLS = 2 * REC_WIDTH + 2 * REC_VWIDTH + 3 * ATT_WIDTH + 2 * D_MODEL
N_GROUPS = 4
EXPERTS_PER_GROUP = 8
N_EXPERTS = N_GROUPS * EXPERTS_PER_GROUP
EXPERT_TOPK = 2
EXPERT_FF = 1024
EXPERT_BLOCK = 128
NORM_EPS = 1e-6

kernel_name = "hgrn2_moba_hier_moe_step"


def rmsnorm(x, g):
    xf = x.astype(jnp.float32)
    y = xf * lax.rsqrt(jnp.mean(xf * xf, axis=-1, keepdims=True) + NORM_EPS)
    return (y * g.astype(jnp.float32)).astype(x.dtype)


def hgrn2_recurrence(q, k, v, log_f, s0):
    B, T, H, DK = q.shape
    DV = v.shape[-1]
    C = math.gcd(T, REC_CHUNK)
    n = T // C

    def chunks(a):
        return a.reshape(B, n, C, H, a.shape[-1]).swapaxes(0, 1)

    causal = jnp.tril(jnp.ones((C, C), dtype=bool))[None, :, :, None, None]

    def step(S, inp):
        qc, kc, vc, lfc = inp
        G = jnp.cumsum(lfc, axis=1)
        decay = jnp.exp(jnp.where(causal, G[:, :, None] - G[:, None, :], -jnp.inf))
        scores = jnp.einsum('bthk,bshk,btshk->bhts', qc, kc, decay)
        o = (jnp.einsum('bhts,bshv->bthv', scores, vc)
             + jnp.einsum('bthk,bhkv->bthv', qc * jnp.exp(G), S))
        G_end = G[:, -1]
        S_new = (jnp.exp(G_end)[..., None] * S
                 + jnp.einsum('bshk,bshv->bhkv', kc * jnp.exp(G_end[:, None] - G), vc))
        return S_new, o

    s_final, o = lax.scan(step, s0, (chunks(q), chunks(k), chunks(v), chunks(log_f)))
    return o.swapaxes(0, 1).reshape(B, T, H, DV), s_final


def hgrn2_branch(q_r, f_r, i_r, g_r, lb, g_norm, s0):
    B, T, _ = q_r.shape
    f32 = jnp.float32
    lbf = lb.astype(f32)
    forget = lbf + (1.0 - lbf) * jax.nn.sigmoid(f_r.astype(f32))
    log_f = jnp.log(forget).reshape(B, T, REC_HEADS, REC_DK)
    key = (1.0 - forget).reshape(B, T, REC_HEADS, REC_DK)
    q = q_r.astype(f32).reshape(B, T, REC_HEADS, REC_DK) * (REC_DK ** -0.5)
    v = i_r.astype(f32).reshape(B, T, REC_HEADS, REC_DV)
    o, s_new = hgrn2_recurrence(q, key, v, log_f, s0.astype(f32))
    o = rmsnorm(o, g_norm) * jax.nn.silu(g_r.astype(f32).reshape(B, T, REC_HEADS, REC_DV))
    return o.reshape(B, T, REC_VWIDTH), s_new


def moba_attention(q, q_pos, means, gather_kv):
    B, T, H, d = q.shape
    f32 = jnp.float32
    NBK = means.shape[1]
    gate = jnp.einsum('bthd,bnhd->bthn', q.astype(f32), means)
    n_full = q_pos // MOBA_BLOCK
    past_ok = jnp.arange(NBK)[None, None, None, :] < n_full[None, :, None, None]
    gate = jnp.where(past_ok, gate, -jnp.inf)
    top_val, top_idx = lax.top_k(gate, min(MOBA_TOPK, NBK))
    own = jnp.broadcast_to(n_full[None, :, None, None], (B, T, H, 1)).astype(top_idx.dtype)
    blocks = jnp.concatenate([top_idx, own], axis=-1)
    block_ok = jnp.concatenate([top_val > -jnp.inf, jnp.ones((B, T, H, 1), dtype=bool)], axis=-1)

    qb = math.gcd(T, max(1, MOBA_QUERY_ROWS // B))
    nq = T // qb

    def to_chunks(a):
        return a.reshape(a.shape[0], nq, qb, *a.shape[2:]).swapaxes(0, 1)

    offs = jnp.arange(MOBA_BLOCK)
    scale = d ** -0.5

    def attend(chunk):
        qc, pc, bc, okc = chunk
        rows = bc[..., None] * MOBA_BLOCK + offs
        kc, vc = gather_kv(rows)
        s = jnp.einsum('bqhd,bqhnkd->bqhnk', qc.astype(f32), kc.astype(f32)) * scale
        ok = okc[..., None] & (rows <= pc[None, :, None, None, None])
        p = jax.nn.softmax(jnp.where(ok, s, -jnp.inf), axis=(-2, -1))
        return jnp.einsum('bqhnk,bqhnkd->bqhd', p, vc.astype(f32))

    o = lax.map(attend, (to_chunks(q), q_pos.reshape(nq, qb), to_chunks(blocks), to_chunks(block_ok)))
    return o.swapaxes(0, 1).reshape(B, T, H * d)


def moba_prompt(q, k, v):
    B, T, H, d = q.shape
    nb = -(-T // MOBA_BLOCK)
    pad = ((0, 0), (0, nb * MOBA_BLOCK - T), (0, 0), (0, 0))
    k_pad, v_pad = jnp.pad(k, pad), jnp.pad(v, pad)
    means = k_pad.astype(jnp.float32).reshape(B, nb, MOBA_BLOCK, H, d).mean(axis=2)
    b_ix = jnp.arange(B)[:, None, None, None, None]
    h_ix = jnp.arange(H)[None, None, :, None, None]

    def gather_kv(rows):
        return k_pad[b_ix, rows, h_ix], v_pad[b_ix, rows, h_ix]

    return moba_attention(q, jnp.arange(T), means, gather_kv)


def moba_sample(q, k, v, cache_k, cache_v, page_table, layer):
    B, T, H, d = q.shape
    page = cache_k.shape[2]
    n_pages = page_table.shape[1]
    past = n_pages * page
    ppb = MOBA_BLOCK // page
    nbp = -(-n_pages // ppb)
    nbk = -(-(past + T) // MOBA_BLOCK)
    page_sums = lax.map(lambda pt: jnp.sum(cache_k[pt, layer].astype(jnp.float32), axis=1), page_table)
    page_sums = jnp.pad(page_sums, ((0, 0), (0, nbp * ppb - n_pages), (0, 0), (0, 0)))
    block_sums = page_sums.reshape(B, nbp, ppb, H, d).sum(axis=2)
    block_sums = jnp.pad(block_sums, ((0, 0), (0, nbk - nbp), (0, 0), (0, 0)))
    q_pos = past + jnp.arange(T)
    block_sums = block_sums.at[:, q_pos // MOBA_BLOCK].add(k.astype(jnp.float32))
    means = block_sums / MOBA_BLOCK
    b_ix = jnp.arange(B)[:, None, None, None, None]
    h_ix = jnp.arange(H)[None, None, :, None, None]

    def gather_kv(rows):
        in_past = (rows < past)[..., None]
        pg = page_table[b_ix, jnp.minimum(rows // page, n_pages - 1)]
        off = rows % page
        new_ix = jnp.clip(rows - past, 0, T - 1)
        kc = jnp.where(in_past, cache_k[pg, layer, off, h_ix], k[b_ix, new_ix, h_ix])
        vc = jnp.where(in_past, cache_v[pg, layer, off, h_ix], v[b_ix, new_ix, h_ix])
        return kc, vc

    return moba_attention(q, q_pos, means, gather_kv)


def routed_experts(x, expert, weight, w_gate, w_up, w_down):
    N, D = x.shape
    K = expert.shape[1]
    M = N * K
    flat_e = expert.reshape(-1)
    flat_tok = jnp.arange(M, dtype=jnp.int32) // K
    flat_w = weight.reshape(-1)
    order = jnp.argsort(flat_e)
    e_sorted = flat_e[order]
    counts = jnp.bincount(flat_e, length=N_EXPERTS)
    padded = (counts + EXPERT_BLOCK - 1) // EXPERT_BLOCK * EXPERT_BLOCK
    pad_end = jnp.cumsum(padded)
    pad_start = pad_end - padded
    start = jnp.cumsum(counts) - counts
    dest = pad_start[e_sorted] + jnp.arange(M) - start[e_sorted]
    n_blocks = -(-M // EXPERT_BLOCK) + N_EXPERTS
    rows = n_blocks * EXPERT_BLOCK
    buf_tok = jnp.zeros((rows,), jnp.int32).at[dest].set(flat_tok[order])
    buf_w = jnp.zeros((rows,), jnp.float32).at[dest].set(flat_w[order])
    block_expert = jnp.minimum(
        jnp.searchsorted(pad_end, jnp.arange(n_blocks) * EXPERT_BLOCK, side='right'), N_EXPERTS - 1)

    def run(blk):
        tok, e = blk
        xb = x[tok]
        hb = jax.nn.silu(xb @ w_gate[e]) * (xb @ w_up[e])
        return hb @ w_down[e]

    yb = lax.map(run, (buf_tok.reshape(n_blocks, EXPERT_BLOCK), block_expert))
    y = jax.ops.segment_sum(yb.reshape(rows, D).astype(jnp.float32) * buf_w[:, None], buf_tok, num_segments=N)
    return y.astype(x.dtype)


def hier_moe(h, w_router_group, w_router_expert, w_gate, w_up, w_down):
    B, T, D = h.shape
    x = h.reshape(B * T, D)
    N = x.shape[0]
    g_prob = jax.nn.softmax((x @ w_router_group).astype(jnp.float32), axis=-1)
    g_w, g_idx = lax.top_k(g_prob, 1)
    e_logits = (x @ w_router_expert).astype(jnp.float32).reshape(N, N_GROUPS, EXPERTS_PER_GROUP)
    e_prob = jax.nn.softmax(e_logits[jnp.arange(N), g_idx[:, 0]], axis=-1)
    e_w, e_idx = lax.top_k(e_prob, EXPERT_TOPK)
    e_w = e_w / jnp.sum(e_w, axis=-1, keepdims=True) * g_w
    expert = g_idx * EXPERTS_PER_GROUP + e_idx
    return routed_experts(x, expert, e_w, w_gate, w_up, w_down).reshape(B, T, D)


def decoder_layer(x, s0, lb, attend, norm_mix, w_in, hgrn_norm, w_rec_out, w_att_out, w_out,
                  norm_ffn, w_router_group, w_router_expert, w_exp_gate, w_exp_up, w_exp_down):
    B, T, _ = x.shape
    h = rmsnorm(x, norm_mix)
    proj = h @ w_in
    sizes = [REC_WIDTH, REC_WIDTH, REC_VWIDTH, REC_VWIDTH, ATT_WIDTH, ATT_WIDTH, ATT_WIDTH, D_MODEL, D_MODEL]
    cuts = [int(c) for c in np.cumsum(sizes)[:-1]]
    q_r, f_r, i_r, g_r, q_a, k_a, v_a, gate_rec, gate_att = jnp.split(proj, cuts, axis=-1)
    o_rec, s_new = hgrn2_branch(q_r, f_r, i_r, g_r, lb, hgrn_norm, s0)
    qh = q_a.reshape(B, T, ATT_HEADS, ATT_DIM)
    kh = k_a.reshape(B, T, ATT_HEADS, ATT_DIM)
    vh = v_a.reshape(B, T, ATT_HEADS, ATT_DIM)
    o_att = attend(qh, kh, vh)
    merged = (jax.nn.sigmoid(gate_rec) * (o_rec.astype(x.dtype) @ w_rec_out)
              + jax.nn.sigmoid(gate_att) * (o_att.astype(x.dtype) @ w_att_out))
    x = x + merged @ w_out
    x = x + hier_moe(rmsnorm(x, norm_ffn), w_router_group, w_router_expert, w_exp_gate, w_exp_up, w_exp_down)
    return x, kh, vh, s_new


def setup_inputs(seed: int = 0) -> dict:
    key = jax.random.key(seed)
    ks = jax.random.split(key, 24)
    n_pages = PAST_LEN // PAGE_SIZE
    n_pool = (DEC_BATCH * n_pages * 5) // 4

    def nrm(k, shape, scale):
        return jax.random.normal(k, shape, jnp.float32) * scale

    return {
        "x_prompt": nrm(ks[0], (BATCH, SEQ, D_MODEL), 1.0),
        "x_sample": nrm(ks[1], (DEC_BATCH, DEC_SEQ, D_MODEL), 1.0),
        "cache_k": nrm(ks[2], (n_pool, DEPTH, PAGE_SIZE, ATT_HEADS, ATT_DIM), 1.0),
        "cache_v": nrm(ks[3], (n_pool, DEPTH, PAGE_SIZE, ATT_HEADS, ATT_DIM), 1.0),
        "state_hgrn": nrm(ks[4], (DEPTH, DEC_BATCH, REC_HEADS, REC_DK, REC_DV), 0.5),
        "page_table": jax.random.permutation(ks[5], n_pool)[:DEC_BATCH * n_pages]
                          .reshape(DEC_BATCH, n_pages).astype(jnp.int32),
        "hgrn_lower_bound": nrm(ks[6], (DEPTH + 1, REC_WIDTH), 0.5),
        "norm_mix": 1.0 + nrm(ks[7], (DEPTH, D_MODEL), 0.02),
        "w_in": nrm(ks[8], (DEPTH, D_MODEL, PROJ_COLS), D_MODEL ** -0.5),
        "hgrn_norm": 1.0 + nrm(ks[9], (DEPTH, REC_DV), 0.02),
        "w_rec_out": nrm(ks[10], (DEPTH, REC_VWIDTH, D_MODEL), REC_VWIDTH ** -0.5),
        "w_att_out": nrm(ks[11], (DEPTH, ATT_WIDTH, D_MODEL), ATT_WIDTH ** -0.5),
        "w_out": nrm(ks[12], (DEPTH, D_MODEL, D_MODEL), D_MODEL ** -0.5),
        "norm_ffn": 1.0 + nrm(ks[13], (DEPTH, D_MODEL), 0.02),
        "w_router_group": nrm(ks[14], (DEPTH, D_MODEL, N_GROUPS), D_MODEL ** -0.5),
        "w_router_expert": nrm(ks[15], (DEPTH, D_MODEL, N_EXPERTS), D_MODEL ** -0.5),
        "w_exp_gate": nrm(ks[16], (DEPTH, N_EXPERTS, D_MODEL, EXPERT_FF), D_MODEL ** -0.5),
        "w_exp_up": nrm(ks[17], (DEPTH, N_EXPERTS, D_MODEL, EXPERT_FF), D_MODEL ** -0.5),
        "w_exp_down": nrm(ks[18], (DEPTH, N_EXPERTS, EXPERT_FF, D_MODEL), EXPERT_FF ** -0.5),
        "norm_final": 1.0 + nrm(ks[19], (D_MODEL,), 0.02),
    }


def reference(x_prompt, x_sample, cache_k, cache_v, state_hgrn, page_table, hgrn_lower_bound, norm_mix,
              w_in, hgrn_norm, w_rec_out, w_att_out, w_out, norm_ffn, w_router_group, w_router_expert,
              w_exp_gate, w_exp_up, w_exp_down, norm_final):
    lower_bounds = jnp.cumsum(jax.nn.softmax(hgrn_lower_bound.astype(jnp.float32), axis=0), axis=0)
    xp, xs = x_prompt, x_sample
    bp = x_prompt.shape[0]
    kp_l, vp_l, sp_l, ks_l, vs_l, ss_l = [], [], [], [], [], []
    for layer in range(DEPTH):
        params = (norm_mix[layer], w_in[layer], hgrn_norm[layer], w_rec_out[layer], w_att_out[layer],
                  w_out[layer], norm_ffn[layer], w_router_group[layer], w_router_expert[layer],
                  w_exp_gate[layer], w_exp_up[layer], w_exp_down[layer])
        lb = lower_bounds[layer]
        s0_prompt = jnp.zeros((bp, REC_HEADS, REC_DK, REC_DV), jnp.float32)
        xp, kp, vp, sp = decoder_layer(xp, s0_prompt, lb, moba_prompt, *params)
        xs, ksm, vsm, ssm = decoder_layer(
            xs, state_hgrn[layer], lb,
            lambda q, k, v, layer=layer: moba_sample(q, k, v, cache_k, cache_v, page_table, layer),
            *params)
        kp_l.append(kp); vp_l.append(vp); sp_l.append(sp)
        ks_l.append(ksm); vs_l.append(vsm); ss_l.append(ssm)
    y_prompt = rmsnorm(xp, norm_final)
    y_sample = rmsnorm(xs, norm_final)
    k_prompt = jnp.stack(kp_l, axis=2)
    v_prompt = jnp.stack(vp_l, axis=2)
    state_prompt = jnp.stack(sp_l, axis=0)
    k_sample = jnp.stack(ks_l, axis=2)
    v_sample = jnp.stack(vs_l, axis=2)
    state_sample = jnp.stack(ss_l, axis=0)
    return (y_prompt, y_sample, k_prompt, v_prompt, state_prompt, k_sample, v_sample, state_sample)
```

```python
import functools
import math

import jax
import jax.numpy as jnp
from jax import lax
from jax.experimental import pallas as pl
from jax.experimental.pallas import tpu as pltpu

F32 = jnp.float32
BF16 = jnp.bfloat16
HIGHEST = lax.Precision.HIGHEST
NEG_INF = float("-inf")

NORM_EPS = 1e-6
LANES = 128
SUBLANES = 8
V7X_VMEM_LIMIT = 56 * 1024 * 1024

REC_HEADS = 8
REC_DK = 128
REC_DV = 128
REC_WIDTH = REC_HEADS * REC_DK
ATT_HEADS = 8
ATT_DIM = 128
ATT_WIDTH = ATT_HEADS * ATT_DIM
MOBA_BLOCK = 256
MOBA_TOPK = 3
N_GROUPS = 4
EXPERTS_PER_GROUP = 8
N_EXPERTS = N_GROUPS * EXPERTS_PER_GROUP
EXPERT_TOPK = 2

REC_CHUNK = 128
REC_SUB = 16
EXPERT_ROWS = 256


def _params(*sem):
    return pltpu.CompilerParams(dimension_semantics=sem, vmem_limit_bytes=V7X_VMEM_LIMIT)


def _tile(n, cap):
    t = max(d for d in range(SUBLANES, cap + 1, SUBLANES) if n % d == 0)
    return t


def _div(x, d):
    assert d & (d - 1) == 0
    return x >> (d.bit_length() - 1)


def _mod(x, d):
    assert d & (d - 1) == 0
    return x & (d - 1)


def _sigmoid(x):
    return 1.0 / (1.0 + jnp.exp(-x))


def _nt_dot(a, b, **kw):
    return lax.dot_general(a, b, (((1,), (1,)), ((), ())), preferred_element_type=F32, **kw)


def _tn_dot(a, b, **kw):
    return lax.dot_general(a, b, (((0,), (0,)), ((), ())), preferred_element_type=F32, **kw)


def _rms_proj_kernel(x_ref, g_ref, w_ref, o_ref, h_ref):
    @pl.when(pl.program_id(1) == 0)
    def _():
        x = x_ref[...]
        ms = jnp.mean(x * x, axis=-1, keepdims=True)
        h_ref[...] = (x * lax.rsqrt(ms + NORM_EPS) * g_ref[...]).astype(BF16)

    o_ref[...] = jnp.dot(h_ref[...], w_ref[...], preferred_element_type=F32)


def rms_proj(x, g, w, *, tm, tn):
    n, d = x.shape
    cols = w.shape[1]
    assert n % tm == 0 and cols % tn == 0
    return pl.pallas_call(
        _rms_proj_kernel,
        grid=(n // tm, cols // tn),
        in_specs=[
            pl.BlockSpec((tm, d), lambda i, j: (i, 0)),
            pl.BlockSpec((1, d), lambda i, j: (0, 0)),
            pl.BlockSpec((d, tn), lambda i, j: (0, j)),
        ],
        out_specs=pl.BlockSpec((tm, tn), lambda i, j: (i, j)),
        out_shape=jax.ShapeDtypeStruct((n, cols), F32),
        scratch_shapes=[pltpu.VMEM((tm, d), BF16)],
        compiler_params=_params("arbitrary", "arbitrary"),
        name="rms_proj",
    )(x, g, w)


def _pad_rows(x, rows):
    if x.shape[0] == rows:
        return x
    return jnp.concatenate([x, jnp.zeros((rows - x.shape[0], x.shape[1]), x.dtype)], axis=0)


def _hgrn_chunk(q, fr, v, g, lb, gn, st, *, rows, sub):
    n_sub = rows // sub
    mm_rows = max(rows, LANES)
    forget = lb + (1.0 - lb) * _sigmoid(fr)
    logf = jnp.log(forget)
    key = 1.0 - forget
    q = q * (REC_DK ** -0.5)

    row_s = lax.broadcasted_iota(jnp.int32, (sub, 1), 0)
    lane_a = lax.broadcasted_iota(jnp.int32, (sub, mm_rows), 1)

    loc = []
    anchor = [jnp.zeros((1, LANES), F32)]
    for i in range(n_sub):
        lf = logf[i * sub:(i + 1) * sub]
        acc = jnp.zeros((sub, LANES), F32)
        for r in range(sub):
            acc = acc + jnp.where(row_s >= r, lf[r:r + 1], 0.0)
        loc.append(acc)
        anchor.append(anchor[i] + acc[sub - 1:sub])
    g_all = jnp.concatenate([loc[i] + anchor[i] for i in range(n_sub)], axis=0)
    g_end = anchor[n_sub]

    a_rows = []
    for i in range(n_sub):
        sl = slice(i * sub, (i + 1) * sub)
        qb, kb, lo = q[sl], key[sl], loc[i]
        qt = qb * jnp.exp(lo)
        if i == 0:
            a_blk = jnp.zeros((sub, mm_rows), F32)
        else:
            kt = key[:i * sub] * jnp.exp(anchor[i] - g_all[:i * sub])
            a_blk = _nt_dot(qt.astype(BF16), _pad_rows(kt, mm_rows).astype(BF16))
        for s in range(sub):
            e = jnp.exp(jnp.minimum(lo - lo[s:s + 1], 0.0))
            col = jnp.sum(qb * (kb[s:s + 1] * e), axis=1, keepdims=True)
            col = jnp.where(row_s >= s, col, 0.0)
            a_blk = jnp.where(lane_a == i * sub + s, col, a_blk)
        a_rows.append(a_blk)
    a_mat = jnp.concatenate(a_rows, axis=0) if n_sub > 1 else a_rows[0]

    v_pad = _pad_rows(v, mm_rows).astype(BF16)
    o = jnp.dot(a_mat.astype(BF16), v_pad, preferred_element_type=F32)
    o = o + _nt_dot((q * jnp.exp(g_all)).astype(BF16), st.astype(BF16))
    k_end = _pad_rows(key * jnp.exp(g_end - g_all), mm_rows).astype(BF16)
    st_new = st * jnp.exp(g_end) + _tn_dot(v_pad, k_end)

    ms = jnp.mean(o * o, axis=-1, keepdims=True)
    o = o * lax.rsqrt(ms + NORM_EPS) * gn
    o = o * (g * _sigmoid(g))
    return o, st_new


def _lower_bound(lbp, layer):
    e = jnp.exp(lbp - jnp.max(lbp, axis=0, keepdims=True))
    sm = e / jnp.sum(e, axis=0, keepdims=True)
    return jnp.sum(sm[:layer + 1], axis=0, keepdims=True)


def _hgrn_kernel(q_ref, f_ref, i_ref, g_ref, lbp_ref, gn_ref, s0_ref, o_ref, s_ref, st_ref, *,
                 layer, heads, rows, sub, n_chunks):
    t_id = pl.program_id(2)

    @pl.when(t_id == 0)
    def _():
        for h in range(heads):
            st_ref[h] = s0_ref[0, h].T

    lb_all = _lower_bound(lbp_ref[...], layer)
    gn = gn_ref[...]

    def chunk(ci, carry):
        r0 = 0 if n_chunks == 1 else pl.multiple_of(ci * rows, rows)
        for h in range(heads):
            ls = slice(h * LANES, (h + 1) * LANES)
            o, st_new = _hgrn_chunk(
                q_ref[pl.ds(r0, rows), ls], f_ref[pl.ds(r0, rows), ls], i_ref[pl.ds(r0, rows), ls],
                g_ref[pl.ds(r0, rows), ls], lb_all[:, ls], gn, st_ref[h], rows=rows, sub=sub)
            o_ref[pl.ds(r0, rows), ls] = o.astype(o_ref.dtype)
            st_ref[h] = st_new
        return carry

    if n_chunks == 1:
        chunk(0, 0)
    else:
        lax.fori_loop(0, n_chunks, chunk, 0)

    @pl.when(t_id == pl.num_programs(2) - 1)
    def _():
        for h in range(heads):
            s_ref[0, h] = st_ref[h].T


def hgrn(proj, lbp, gn, s0, *, layer, row0, seq, heads, rows, sub, tile):
    batch = s0.shape[0]
    assert seq % tile == 0 and tile % rows == 0 and rows % sub == 0 and row0 % tile == 0
    assert REC_HEADS % heads == 0
    w = heads * LANES
    nt = seq // tile
    rb0 = row0 // tile
    cb = REC_WIDTH // w

    def seg(k):
        return pl.BlockSpec((tile, w), lambda b, hb, t: (rb0 + b * nt + t, k * cb + hb))

    kern = functools.partial(_hgrn_kernel, layer=layer, heads=heads, rows=rows, sub=sub, n_chunks=tile // rows)
    return pl.pallas_call(
        kern,
        grid=(batch, REC_HEADS // heads, nt),
        in_specs=[
            seg(0), seg(1), seg(2), seg(3),
            pl.BlockSpec((lbp.shape[0], w), lambda b, hb, t: (0, hb)),
            pl.BlockSpec((1, REC_DV), lambda b, hb, t: (0, 0)),
            pl.BlockSpec((1, heads, REC_DK, REC_DV), lambda b, hb, t: (b, hb, 0, 0)),
        ],
        out_specs=[
            pl.BlockSpec((tile, w), lambda b, hb, t: (b * nt + t, hb)),
            pl.BlockSpec((1, heads, REC_DK, REC_DV), lambda b, hb, t: (b, hb, 0, 0)),
        ],
        out_shape=[
            jax.ShapeDtypeStruct((batch * seq, REC_WIDTH), BF16),
            jax.ShapeDtypeStruct((batch, REC_HEADS, REC_DK, REC_DV), F32),
        ],
        scratch_shapes=[pltpu.VMEM((heads, REC_DV, REC_DK), F32)],
        compiler_params=_params("arbitrary", "arbitrary", "arbitrary"),
        name="hgrn",
    )(proj, proj, proj, proj, lbp, gn, s0)


def _topk_mask(gate, k):
    width = gate.shape[1]
    lane = lax.broadcasted_iota(jnp.int32, gate.shape, 1).astype(F32)
    sel = jnp.zeros(gate.shape, F32)
    for _ in range(k):
        m = jnp.max(gate, axis=1, keepdims=True)
        first = jnp.min(jnp.where(gate == m, lane, float(width)), axis=1, keepdims=True)
        hit = lane == first
        sel = jnp.where(hit & (m > NEG_INF), 1.0, sel)
        gate = jnp.where(hit, NEG_INF, gate)
    return sel


def _moba_prompt_kernel(q_ref, k_ref, v_ref, o_ref, means_ref, *, n_blocks):
    qi = pl.program_id(2)
    blk = MOBA_BLOCK
    scale = ATT_DIM ** -0.5

    @pl.when(qi == 0)
    def _():
        means_ref[...] = jnp.zeros(means_ref.shape, F32)
        for j in range(n_blocks):
            means_ref[j:j + 1, :] = jnp.sum(k_ref[j * blk:(j + 1) * blk, :], axis=0, keepdims=True) * (1.0 / blk)

    q = q_ref[...]
    lane = lax.broadcasted_iota(jnp.int32, (blk, LANES), 1)
    gate = _nt_dot(q, means_ref[...], precision=HIGHEST)
    sel = _topk_mask(jnp.where(lane < qi, gate, NEG_INF), MOBA_TOPK)

    qb = q.astype(BF16)
    own = pl.multiple_of(qi * blk, blk)
    row = lax.broadcasted_iota(jnp.int32, (blk, blk), 0)
    col = lax.broadcasted_iota(jnp.int32, (blk, blk), 1)
    s = _nt_dot(qb, k_ref[pl.ds(own, blk), :].astype(BF16)) * scale
    s = jnp.where(col <= row, s, NEG_INF)
    m = jnp.max(s, axis=1, keepdims=True)
    p = jnp.exp(s - m)
    l = jnp.sum(p, axis=1, keepdims=True)
    acc = jnp.dot(p.astype(BF16), v_ref[pl.ds(own, blk), :].astype(BF16), preferred_element_type=F32)

    def past(j, carry):
        m, l, acc = carry
        r0 = pl.multiple_of(j * blk, blk)
        chosen = jnp.max(jnp.where(lane == j, sel, 0.0), axis=1, keepdims=True) > 0.0
        s = _nt_dot(qb, k_ref[pl.ds(r0, blk), :].astype(BF16)) * scale
        s = jnp.where(chosen, s, NEG_INF)
        m_new = jnp.maximum(m, jnp.max(s, axis=1, keepdims=True))
        alpha = jnp.exp(m - m_new)
        p = jnp.exp(s - m_new)
        l = alpha * l + jnp.sum(p, axis=1, keepdims=True)
        acc = alpha * acc + jnp.dot(p.astype(BF16), v_ref[pl.ds(r0, blk), :].astype(BF16),
                                    preferred_element_type=F32)
        return m_new, l, acc

    m, l, acc = lax.fori_loop(0, qi, past, (m, l, acc))
    o_ref[...] = (acc / l).astype(o_ref.dtype)


def moba_prompt(proj, *, batch, seq, col0):
    assert seq % MOBA_BLOCK == 0 and col0 % LANES == 0
    nb = seq // MOBA_BLOCK
    assert nb <= LANES
    c0 = col0 // LANES
    kern = functools.partial(_moba_prompt_kernel, n_blocks=nb)
    return pl.pallas_call(
        kern,
        grid=(batch, ATT_HEADS, nb),
        in_specs=[
            pl.BlockSpec((MOBA_BLOCK, ATT_DIM), lambda b, h, i: (b * nb + i, c0 + h)),
            pl.BlockSpec((seq, ATT_DIM), lambda b, h, i: (b, c0 + ATT_HEADS + h)),
            pl.BlockSpec((seq, ATT_DIM), lambda b, h, i: (b, c0 + 2 * ATT_HEADS + h)),
        ],
        out_specs=pl.BlockSpec((MOBA_BLOCK, ATT_DIM), lambda b, h, i: (b * nb + i, h)),
        out_shape=jax.ShapeDtypeStruct((batch * seq, ATT_WIDTH), BF16),
        scratch_shapes=[pltpu.VMEM((LANES, ATT_DIM), F32)],
        compiler_params=_params("arbitrary", "arbitrary", "arbitrary"),
        name="moba_prompt",
    )(proj, proj, proj)


def _head_major(ref, t_len):
    return jnp.concatenate([ref[:, h * ATT_DIM:(h + 1) * ATT_DIM] for h in range(ATT_HEADS)], axis=0)


def _moba_gate_kernel(pt_ref, q_ref, *refs, t_len, pages_per_step, n_past_blocks, page):
    k_refs = refs[:pages_per_step]
    sel_ref = refs[pages_per_step]
    means_ref = refs[pages_per_step + 1]
    g = pl.program_id(1)
    ppb = MOBA_BLOCK // page
    for u in range(pages_per_step // ppb):
        tot = jnp.sum(k_refs[u * ppb][...], axis=0)
        for w in range(1, ppb):
            tot = tot + jnp.sum(k_refs[u * ppb + w][...], axis=0)
        r0 = pl.multiple_of((g * (pages_per_step // ppb) + u) * ATT_HEADS, ATT_HEADS)
        means_ref[pl.ds(r0, ATT_HEADS), :] = tot * (1.0 / MOBA_BLOCK)

    @pl.when(g == pl.num_programs(1) - 1)
    def _():
        qf = _head_major(q_ref, t_len)
        rows = ATT_HEADS * t_len
        cols = n_past_blocks * ATT_HEADS
        gate = _nt_dot(qf, means_ref[...], precision=HIGHEST)
        rh = _div(lax.broadcasted_iota(jnp.int32, (rows, cols), 0), t_len)
        ch = _mod(lax.broadcasted_iota(jnp.int32, (rows, cols), 1), ATT_HEADS)
        sel_ref[...] = _topk_mask(jnp.where(rh == ch, gate, NEG_INF), MOBA_TOPK)


def _moba_sample_kernel(pt_ref, q_ref, kn_ref, vn_ref, sel_ref, *refs, t_len, ppb, page):
    k_refs = refs[:ppb]
    v_refs = refs[ppb:2 * ppb]
    o_ref = refs[2 * ppb]
    qs_ref, m_ref, l_ref, acc_ref = refs[2 * ppb + 1:]
    j = pl.program_id(1)
    rows = ATT_HEADS * t_len
    scale = ATT_DIM ** -0.5

    @pl.when(j == 0)
    def _():
        qf = _head_major(q_ref, t_len).astype(BF16)
        qs_ref[...] = qf
        kn = _head_major(kn_ref, t_len).astype(BF16)
        vn = _head_major(vn_ref, t_len).astype(BF16)
        s = _nt_dot(qf, kn) * scale
        r = lax.broadcasted_iota(jnp.int32, (rows, rows), 0)
        c = lax.broadcasted_iota(jnp.int32, (rows, rows), 1)
        ok = (_div(r, t_len) == _div(c, t_len)) & (_mod(c, t_len) <= _mod(r, t_len))
        s = jnp.where(ok, s, NEG_INF)
        m = jnp.max(s, axis=1, keepdims=True)
        p = jnp.exp(s - m)
        m_ref[...] = m
        l_ref[...] = jnp.sum(p, axis=1, keepdims=True)
        acc_ref[...] = jnp.dot(p.astype(BF16), vn, preferred_element_type=F32)

    sel = sel_ref[...]
    cb = _div(lax.broadcasted_iota(jnp.int32, sel.shape, 1), ATT_HEADS)
    chosen = jnp.max(jnp.where(cb == j, sel, 0.0), axis=1, keepdims=True) > 0.0
    width = page * ATT_HEADS
    rh = _div(lax.broadcasted_iota(jnp.int32, (rows, width), 0), t_len)
    ch = _mod(lax.broadcasted_iota(jnp.int32, (rows, width), 1), ATT_HEADS)
    ok = (rh == ch) & chosen
    qf = qs_ref[...]
    for u in range(ppb):
        kf = k_refs[u][...].reshape(width, ATT_DIM).astype(BF16)
        vf = v_refs[u][...].reshape(width, ATT_DIM).astype(BF16)
        s = jnp.where(ok, _nt_dot(qf, kf) * scale, NEG_INF)
        m = m_ref[...]
        m_new = jnp.maximum(m, jnp.max(s, axis=1, keepdims=True))
        alpha = jnp.exp(m - m_new)
        p = jnp.exp(s - m_new)
        m_ref[...] = m_new
        l_ref[...] = alpha * l_ref[...] + jnp.sum(p, axis=1, keepdims=True)
        acc_ref[...] = alpha * acc_ref[...] + jnp.dot(p.astype(BF16), vf, preferred_element_type=F32)

    @pl.when(j == pl.num_programs(1) - 1)
    def _():
        o = acc_ref[...] / l_ref[...]
        for h in range(ATT_HEADS):
            o_ref[:, h * ATT_DIM:(h + 1) * ATT_DIM] = o[h * t_len:(h + 1) * t_len].astype(o_ref.dtype)


def moba_sample(proj, cache_k, cache_v, page_table, *, layer, row0, t_len, col0):
    batch, n_pages = page_table.shape
    page = cache_k.shape[2]
    ppb = MOBA_BLOCK // page
    past = n_pages * page
    assert MOBA_BLOCK % page == 0 and n_pages % ppb == 0
    assert past % MOBA_BLOCK == 0 and t_len <= MOBA_BLOCK
    assert row0 % t_len == 0 and t_len % SUBLANES == 0 and col0 % ATT_WIDTH == 0
    n_past_blocks = past // MOBA_BLOCK
    rows = ATT_HEADS * t_len
    rb0 = row0 // t_len
    c0 = col0 // ATT_WIDTH
    pps = 8
    assert n_pages % pps == 0 and pps % ppb == 0

    def page_spec(u, per_step):
        return pl.BlockSpec((None, None, page, ATT_HEADS, ATT_DIM),
                            lambda b, g, pt: (pt[b, g * per_step + u], layer, 0, 0, 0))

    def new_spec(k):
        return pl.BlockSpec((t_len, ATT_WIDTH), lambda b, g, pt: (rb0 + b, c0 + k))

    gate_kern = functools.partial(_moba_gate_kernel, t_len=t_len, pages_per_step=pps,
                                  n_past_blocks=n_past_blocks, page=page)
    sel = pl.pallas_call(
        gate_kern,
        grid_spec=pltpu.PrefetchScalarGridSpec(
            num_scalar_prefetch=1,
            grid=(batch, n_pages // pps),
            in_specs=[new_spec(0)] + [page_spec(u, pps) for u in range(pps)],
            out_specs=pl.BlockSpec((None, rows, n_past_blocks * ATT_HEADS), lambda b, g, pt: (b, 0, 0)),
            scratch_shapes=[pltpu.VMEM((n_past_blocks * ATT_HEADS, ATT_DIM), F32)],
        ),
        out_shape=jax.ShapeDtypeStruct((batch, rows, n_past_blocks * ATT_HEADS), F32),
        compiler_params=_params("arbitrary", "arbitrary"),
        name="moba_gate",
    )(page_table, proj, *([cache_k] * pps))

    att_kern = functools.partial(_moba_sample_kernel, t_len=t_len, ppb=ppb, page=page)
    return pl.pallas_call(
        att_kern,
        grid_spec=pltpu.PrefetchScalarGridSpec(
            num_scalar_prefetch=1,
            grid=(batch, n_past_blocks),
            in_specs=[new_spec(0), new_spec(1), new_spec(2),
                      pl.BlockSpec((None, rows, n_past_blocks * ATT_HEADS), lambda b, g, pt: (b, 0, 0))]
            + [page_spec(u, ppb) for u in range(ppb)] * 2,
            out_specs=pl.BlockSpec((t_len, ATT_WIDTH), lambda b, g, pt: (b, 0)),
            scratch_shapes=[pltpu.VMEM((rows, ATT_DIM), BF16), pltpu.VMEM((rows, 1), F32),
                            pltpu.VMEM((rows, 1), F32), pltpu.VMEM((rows, ATT_DIM), F32)],
        ),
        out_shape=jax.ShapeDtypeStruct((batch * t_len, ATT_WIDTH), BF16),
        compiler_params=_params("arbitrary", "arbitrary"),
        name="moba_sample",
    )(page_table, proj, proj, proj, sel, *([cache_k] * ppb), *([cache_v] * ppb))


def _merge_kernel(orec_ref, oatt_ref, wr_ref, wa_ref, gr_ref, ga_ref, o_ref):
    rec = jnp.dot(orec_ref[...], wr_ref[...], preferred_element_type=F32)
    att = jnp.dot(oatt_ref[...], wa_ref[...], preferred_element_type=F32)
    o_ref[...] = (_sigmoid(gr_ref[...]) * rec + _sigmoid(ga_ref[...]) * att).astype(o_ref.dtype)


def merge(o_rec, o_att, w_rec, w_att, proj, *, col_rec, col_att, tm, tn):
    n, d = o_rec.shape[0], w_rec.shape[1]
    assert n % tm == 0 and d % tn == 0 and col_rec % tn == 0 and col_att % tn == 0
    cr, ca = col_rec // tn, col_att // tn
    return pl.pallas_call(
        _merge_kernel,
        grid=(n // tm, d // tn),
        in_specs=[
            pl.BlockSpec((tm, o_rec.shape[1]), lambda i, j: (i, 0)),
            pl.BlockSpec((tm, o_att.shape[1]), lambda i, j: (i, 0)),
            pl.BlockSpec((w_rec.shape[0], tn), lambda i, j: (0, j)),
            pl.BlockSpec((w_att.shape[0], tn), lambda i, j: (0, j)),
            pl.BlockSpec((tm, tn), lambda i, j: (i, cr + j)),
            pl.BlockSpec((tm, tn), lambda i, j: (i, ca + j)),
        ],
        out_specs=pl.BlockSpec((tm, tn), lambda i, j: (i, j)),
        out_shape=jax.ShapeDtypeStruct((n, d), BF16),
        compiler_params=_params("arbitrary", "arbitrary"),
        name="merge",
    )(o_rec, o_att, w_rec, w_att, proj, proj)


def _out_router_kernel(m_ref, x_ref, w_ref, g_ref, wr_ref, x2_ref, hn_ref, ids_ref, wts_ref):
    x2 = x_ref[...] + jnp.dot(m_ref[...], w_ref[...], preferred_element_type=F32)
    x2_ref[...] = x2
    ms = jnp.mean(x2 * x2, axis=-1, keepdims=True)
    hn = x2 * lax.rsqrt(ms + NORM_EPS) * g_ref[...]
    hn_ref[...] = hn
    logits = jnp.dot(hn, wr_ref[...], preferred_element_type=F32, precision=HIGHEST)
    lane = lax.broadcasted_iota(jnp.int32, logits.shape, 1).astype(F32)
    big = float(LANES)
    gl = jnp.where(lane < N_GROUPS, logits, NEG_INF)
    gmax = jnp.max(gl, axis=1, keepdims=True)
    g_idx = jnp.min(jnp.where(gl == gmax, lane, big), axis=1, keepdims=True)
    g_w = 1.0 / jnp.sum(jnp.exp(gl - gmax), axis=1, keepdims=True)
    lo = N_GROUPS + g_idx * EXPERTS_PER_GROUP
    el = jnp.where((lane >= lo) & (lane < lo + EXPERTS_PER_GROUP), logits, NEG_INF)
    e1 = jnp.max(el, axis=1, keepdims=True)
    i1 = jnp.min(jnp.where(el == e1, lane, big), axis=1, keepdims=True)
    el2 = jnp.where(lane == i1, NEG_INF, el)
    e2 = jnp.max(el2, axis=1, keepdims=True)
    i2 = jnp.min(jnp.where(el2 == e2, lane, big), axis=1, keepdims=True)
    den = jnp.sum(jnp.exp(el - e1), axis=1, keepdims=True)
    p1 = 1.0 / den
    p2 = jnp.exp(e2 - e1) / den
    tot = p1 + p2
    two = lax.broadcasted_iota(jnp.int32, (logits.shape[0], EXPERT_TOPK), 1)
    ids_ref[...] = jnp.where(two == 0, i1, i2).astype(jnp.int32) - N_GROUPS
    wts_ref[...] = jnp.where(two == 0, p1 / tot * g_w, p2 / tot * g_w)


def out_router(merged, x, w_out, g_ffn, w_router, *, tm):
    n, d = x.shape
    assert n % tm == 0
    return pl.pallas_call(
        _out_router_kernel,
        grid=(n // tm,),
        in_specs=[
            pl.BlockSpec((tm, d), lambda i: (i, 0)),
            pl.BlockSpec((tm, d), lambda i: (i, 0)),
            pl.BlockSpec((d, d), lambda i: (0, 0)),
            pl.BlockSpec((1, d), lambda i: (0, 0)),
            pl.BlockSpec((d, LANES), lambda i: (0, 0)),
        ],
        out_specs=[
            pl.BlockSpec((tm, d), lambda i: (i, 0)),
            pl.BlockSpec((tm, d), lambda i: (i, 0)),
            pl.BlockSpec((tm, EXPERT_TOPK), lambda i: (i, 0)),
            pl.BlockSpec((tm, EXPERT_TOPK), lambda i: (i, 0)),
        ],
        out_shape=[
            jax.ShapeDtypeStruct((n, d), F32),
            jax.ShapeDtypeStruct((n, d), F32),
            jax.ShapeDtypeStruct((n, EXPERT_TOPK), jnp.int32),
            jax.ShapeDtypeStruct((n, EXPERT_TOPK), F32),
        ],
        compiler_params=_params("arbitrary"),
        name="out_router",
    )(merged, x, w_out, g_ffn, w_router)


def _row_copy(src_ref, dst_ref, src_row, dst_row, sem):
    return pltpu.make_async_copy(src_ref.at[pl.ds(src_row, 1)], dst_ref.at[pl.ds(dst_row, 1)], sem)


def _gather_rows_kernel(tok_ref, src_ref, o_ref, sem, *, bm):
    base = pl.program_id(0) * bm

    def start(r, c):
        _row_copy(src_ref, o_ref, tok_ref[base + r], r, sem).start()
        return c

    def wait(r, c):
        _row_copy(src_ref, o_ref, 0, r, sem).wait()
        return c

    lax.fori_loop(0, bm, start, 0)
    lax.fori_loop(0, bm, wait, 0)


def gather_rows(src, tok, *, bm):
    rows = tok.shape[0]
    d = src.shape[1]
    assert rows % bm == 0
    return pl.pallas_call(
        functools.partial(_gather_rows_kernel, bm=bm),
        grid_spec=pltpu.PrefetchScalarGridSpec(
            num_scalar_prefetch=1,
            grid=(rows // bm,),
            in_specs=[pl.BlockSpec(memory_space=pl.ANY)],
            out_specs=pl.BlockSpec((bm, d), lambda i, tok: (i, 0)),
            scratch_shapes=[pltpu.SemaphoreType.DMA],
        ),
        out_shape=jax.ShapeDtypeStruct((rows, d), src.dtype),
        compiler_params=_params("arbitrary"),
        name="gather_rows",
    )(tok, src)


def _expert_kernel(be_ref, nu_ref, x_ref, wg_ref, wu_ref, wd_ref, o_ref):
    i = pl.program_id(0)

    @pl.when(i < nu_ref[0])
    def _():
        xb = x_ref[...].astype(BF16)
        a = jnp.dot(xb, wg_ref[...], preferred_element_type=F32)
        b = jnp.dot(xb, wu_ref[...], preferred_element_type=F32)
        hb = (a * _sigmoid(a) * b).astype(BF16)
        o_ref[...] = jnp.dot(hb, wd_ref[...], preferred_element_type=F32)

    @pl.when(i >= nu_ref[0])
    def _():
        o_ref[...] = jnp.zeros(o_ref.shape, o_ref.dtype)


def expert_ffn(xs, block_expert, n_used, w_gate, w_up, w_down, *, bm):
    rows, d = xs.shape
    ff = w_gate.shape[2]
    n_blocks = rows // bm

    def live(i, nu):
        return jnp.minimum(i, jnp.maximum(nu[0] - 1, 0))

    return pl.pallas_call(
        _expert_kernel,
        grid_spec=pltpu.PrefetchScalarGridSpec(
            num_scalar_prefetch=2,
            grid=(n_blocks,),
            in_specs=[
                pl.BlockSpec((bm, d), lambda i, be, nu: (live(i, nu), 0)),
                pl.BlockSpec((None, d, ff), lambda i, be, nu: (be[live(i, nu)], 0, 0)),
                pl.BlockSpec((None, d, ff), lambda i, be, nu: (be[live(i, nu)], 0, 0)),
                pl.BlockSpec((None, ff, d), lambda i, be, nu: (be[live(i, nu)], 0, 0)),
            ],
            out_specs=pl.BlockSpec((bm, d), lambda i, be, nu: (i, 0)),
        ),
        out_shape=jax.ShapeDtypeStruct((rows, d), F32),
        compiler_params=_params("arbitrary"),
        name="expert_ffn",
    )(block_expert, n_used, xs, w_gate, w_up, w_down)


def _combine_kernel(p0_ref, p1_ref, yb_ref, x2_ref, w_ref, g_ref, o_ref, r0_ref, r1_ref, sem, *, tm):
    base = pl.program_id(0) * tm

    def start(r, c):
        _row_copy(yb_ref, r0_ref, p0_ref[base + r], r, sem).start()
        _row_copy(yb_ref, r1_ref, p1_ref[base + r], r, sem).start()
        return c

    def wait(r, c):
        _row_copy(yb_ref, r0_ref, 0, r, sem).wait()
        _row_copy(yb_ref, r1_ref, 0, r, sem).wait()
        return c

    lax.fori_loop(0, tm, start, 0)
    lax.fori_loop(0, tm, wait, 0)
    w = w_ref[...]
    y = x2_ref[...] + (r0_ref[...] * w[:, 0:1] + r1_ref[...] * w[:, 1:2])
    ms = jnp.mean(y * y, axis=-1, keepdims=True)
    o_ref[...] = y * lax.rsqrt(ms + NORM_EPS) * g_ref[...]


def combine(yb, pos0, pos1, x2, wts, g_final, *, tm):
    n, d = x2.shape
    assert n % tm == 0
    return pl.pallas_call(
        functools.partial(_combine_kernel, tm=tm),
        grid_spec=pltpu.PrefetchScalarGridSpec(
            num_scalar_prefetch=2,
            grid=(n // tm,),
            in_specs=[
                pl.BlockSpec(memory_space=pl.ANY),
                pl.BlockSpec((tm, d), lambda i, p0, p1: (i, 0)),
                pl.BlockSpec((tm, EXPERT_TOPK), lambda i, p0, p1: (i, 0)),
                pl.BlockSpec((1, d), lambda i, p0, p1: (0, 0)),
            ],
            out_specs=pl.BlockSpec((tm, d), lambda i, p0, p1: (i, 0)),
            scratch_shapes=[pltpu.VMEM((tm, d), F32), pltpu.VMEM((tm, d), F32), pltpu.SemaphoreType.DMA],
        ),
        out_shape=jax.ShapeDtypeStruct((n, d), F32),
        compiler_params=_params("arbitrary"),
        name="combine",
    )(pos0, pos1, yb, x2, wts, g_final)


def _routing_tables(ids, bm):
    n, k = ids.shape
    m = n * k
    flat_e = ids.reshape(-1)
    order = jnp.argsort(flat_e)
    e_sorted = flat_e[order]
    counts = jnp.bincount(flat_e, length=N_EXPERTS)
    padded = (counts + bm - 1) // bm * bm
    pad_end = jnp.cumsum(padded)
    pad_start = pad_end - padded
    start = jnp.cumsum(counts) - counts
    dest = (pad_start[e_sorted] + jnp.arange(m) - start[e_sorted]).astype(jnp.int32)
    n_blocks = -(-m // bm) + N_EXPERTS
    buf_tok = jnp.zeros((n_blocks * bm,), jnp.int32).at[dest].set((order // k).astype(jnp.int32))
    pos = jnp.zeros((m,), jnp.int32).at[order].set(dest).reshape(n, k)
    block_expert = jnp.minimum(
        jnp.searchsorted(pad_end, jnp.arange(n_blocks) * bm, side="right"), N_EXPERTS - 1).astype(jnp.int32)
    n_used = (pad_end[-1:] // bm).astype(jnp.int32)
    return buf_tok, block_expert, pos, n_used


def kernel(x_prompt, x_sample, cache_k, cache_v, state_hgrn, page_table, hgrn_lower_bound, norm_mix, w_in,
           hgrn_norm, w_rec_out, w_att_out, w_out, norm_ffn, w_router_group, w_router_expert, w_exp_gate,
           w_exp_up, w_exp_down, norm_final):
    bp, sp, d = x_prompt.shape
    bs, ss, _ = x_sample.shape
    depth = w_in.shape[0]
    assert depth == 1, "the final norm is fused into the last stage of the single layer"
    n_p, n_s = bp * sp, bs * ss
    n = n_p + n_s
    tm = _tile(n, 1024)
    x = jnp.concatenate([x_prompt.reshape(n_p, d), x_sample.reshape(n_s, d)], axis=0)

    col_att = 4 * REC_WIDTH
    col_gate_rec = col_att + 3 * ATT_WIDTH
    col_gate_att = col_gate_rec + d
    outs = {k: [] for k in ("kp", "vp", "sp", "ks", "vs", "ss")}
    for layer in range(depth):
        proj = rms_proj(x, norm_mix[layer][None], w_in[layer].astype(BF16), tm=tm, tn=1024)

        gn = hgrn_norm[layer][None]
        s0_prompt = jnp.zeros((bp, REC_HEADS, REC_DK, REC_DV), F32)
        o_rec_p, st_p = hgrn(proj, hgrn_lower_bound, gn, s0_prompt, layer=layer, row0=0, seq=sp, heads=1,
                             rows=REC_CHUNK, sub=REC_SUB, tile=_tile(sp, 512))
        o_rec_s, st_s = hgrn(proj, hgrn_lower_bound, gn, state_hgrn[layer], layer=layer, row0=n_p, seq=ss,
                             heads=REC_HEADS, rows=ss, sub=SUBLANES, tile=ss)

        o_att_p = moba_prompt(proj, batch=bp, seq=sp, col0=col_att)
        o_att_s = moba_sample(proj, cache_k, cache_v, page_table, layer=layer, row0=n_p, t_len=ss, col0=col_att)

        o_rec = jnp.concatenate([o_rec_p, o_rec_s], axis=0)
        o_att = jnp.concatenate([o_att_p, o_att_s], axis=0)
        merged = merge(o_rec, o_att, w_rec_out[layer].astype(BF16), w_att_out[layer].astype(BF16), proj,
                       col_rec=col_gate_rec, col_att=col_gate_att, tm=tm, tn=1024)

        w_router = jnp.concatenate([w_router_group[layer], w_router_expert[layer]], axis=1)
        w_router = jnp.pad(w_router, ((0, 0), (0, LANES - w_router.shape[1])))
        x2, hn, ids, wts = out_router(merged, x, w_out[layer].astype(BF16), norm_ffn[layer][None], w_router,
                                      tm=_tile(n, 256))

        buf_tok, block_expert, pos, n_used = _routing_tables(ids, EXPERT_ROWS)
        xs = gather_rows(hn, buf_tok, bm=EXPERT_ROWS)
        yb = expert_ffn(xs, block_expert, n_used, w_exp_gate[layer].astype(BF16), w_exp_up[layer].astype(BF16),
                        w_exp_down[layer].astype(BF16), bm=EXPERT_ROWS)
        x = combine(yb, pos[:, 0], pos[:, 1], x2, wts, norm_final[None], tm=_tile(n, 128))

        k_cols = proj[:, col_att + ATT_WIDTH:col_att + 2 * ATT_WIDTH]
        v_cols = proj[:, col_att + 2 * ATT_WIDTH:col_att + 3 * ATT_WIDTH]
        outs["kp"].append(k_cols[:n_p].reshape(bp, sp, ATT_HEADS, ATT_DIM))
        outs["vp"].append(v_cols[:n_p].reshape(bp, sp, ATT_HEADS, ATT_DIM))
        outs["ks"].append(k_cols[n_p:].reshape(bs, ss, ATT_HEADS, ATT_DIM))
        outs["vs"].append(v_cols[n_p:].reshape(bs, ss, ATT_HEADS, ATT_DIM))
        outs["sp"].append(st_p)
        outs["ss"].append(st_s)

    y_prompt = x[:n_p].reshape(bp, sp, d)
    y_sample = x[n_p:].reshape(bs, ss, d)
    return (y_prompt, y_sample, jnp.stack(outs["kp"], axis=2), jnp.stack(outs["vp"], axis=2),
            jnp.stack(outs["sp"], axis=0), jnp.stack(outs["ks"], axis=2), jnp.stack(outs["vs"], axis=2),
            jnp.stack(outs["ss"], axis=0))
```
